```python
import math
import jax, jax.numpy as jnp
from jax import lax
import numpy as np

D_MODEL = 4096
BATCH = 4
SEQ = 2048
DEPTH = 1
DEC_BATCH = 128
DEC_SEQ = 4
PAST_LEN = 16384
PAGE_SIZE = 128

MLA_HEADS = 16
MLA_NOPE = 128
MLA_ROPE = 64
MLA_VDIM = 128
MLA_Q_RANK = 1024
MLA_KV_RANK = 512
MLA_SCALE = (MLA_NOPE + MLA_ROPE) ** -0.5
DSA_HEADS = 16
DSA_KV_HEADS = 2
DSA_HDIM = 128
DSA_SCALE = DSA_HDIM ** -0.5
IDX_HEADS = 32
IDX_DIM = 128
TOPK_MAX = 256
N_BUCKETS = 32
MAX_DISTANCE = 128
PEER_HEADS = 8
PEER_KEYS = 128
PEER_EXPERTS = PEER_KEYS * PEER_KEYS
PEER_QDIM = 256
PEER_HALF = PEER_QDIM // 2
PEER_TOPK = 16
PEER_BLOCK = 64
Q_BLOCK = 128
ROPE_THETA = 10000.0
EPS = 1e-6
NEG = -1e30

SPLIT_SIZES = (MLA_Q_RANK, MLA_KV_RANK, MLA_ROPE,
               DSA_HEADS * DSA_HDIM, DSA_KV_HEADS * DSA_HDIM, DSA_KV_HEADS * DSA_HDIM,
               IDX_HEADS * IDX_DIM, IDX_DIM, IDX_HEADS,
               D_MODEL, D_MODEL)
N_IN = sum(SPLIT_SIZES)

kernel_name = 'hybrid_mla_dsa_peer_step'


def rmsnorm(x, g):
    xf = x.astype(jnp.float32)
    y = xf * lax.rsqrt(jnp.mean(xf * xf, axis=-1, keepdims=True) + EPS)
    return (y * g.astype(jnp.float32)).astype(x.dtype)


def rope(x, pos):
    half = x.shape[-1] // 2
    inv = jnp.power(ROPE_THETA, -jnp.arange(half, dtype=jnp.float32) / half)
    ang = pos.astype(jnp.float32)[:, None] * inv[None, :]
    shape = (pos.shape[0],) + (1,) * (x.ndim - 3) + (half,)
    cos = jnp.cos(ang).reshape(shape)
    sin = jnp.sin(ang).reshape(shape)
    xf = x.astype(jnp.float32)
    x1, x2 = xf[..., :half], xf[..., half:]
    return jnp.concatenate([x1 * cos - x2 * sin, x1 * sin + x2 * cos], axis=-1).astype(x.dtype)


def mla_qk_norm(nope, rot, g_nope, g_pair):
    nf = nope.astype(jnp.float32)
    rf = rot.astype(jnp.float32)
    ms = (jnp.sum(nf * nf, -1, keepdims=True) + jnp.sum(rf * rf, -1, keepdims=True)) / (MLA_NOPE + MLA_ROPE)
    r = lax.rsqrt(ms + EPS)
    g_rot = jnp.concatenate([g_pair, g_pair]).astype(jnp.float32)
    return jnp.concatenate([nf * r * g_nope.astype(jnp.float32), rf * r * g_rot], axis=-1).astype(nope.dtype)


def mla_keys(ckv, krope, w_ukv, gk_nope, gk_pair):
    kv = jnp.einsum('...sc,cf->...sf', ckv, w_ukv)
    kv = kv.reshape(ckv.shape[:-1] + (MLA_HEADS, MLA_NOPE + MLA_VDIM))
    k_nope, v = kv[..., :MLA_NOPE], kv[..., MLA_NOPE:]
    k_rot = jnp.broadcast_to(krope[..., None, :], k_nope.shape[:-1] + (MLA_ROPE,))
    return mla_qk_norm(k_nope, k_rot, gk_nope, gk_pair), v


def t5_bias(dist, table):
    n = jnp.maximum(dist, 0)
    exact = N_BUCKETS // 2
    nf = jnp.maximum(n, exact).astype(jnp.float32)
    large = exact + (jnp.log(nf / exact) / math.log(MAX_DISTANCE / exact) * (N_BUCKETS - exact)).astype(jnp.int32)
    bucket = jnp.where(n < exact, n, jnp.minimum(large, N_BUCKETS - 1))
    return jnp.take(table, bucket, axis=0).astype(jnp.float32)


def index_scores(qi, ki, wi):
    s = jnp.einsum('...thd,...sd->...ths', qi, ki).astype(jnp.float32) * (IDX_DIM ** -0.5)
    return jnp.einsum('...ths,...th->...ts', jax.nn.relu(s), wi.astype(jnp.float32) * (IDX_HEADS ** -0.5))


def select_keys(score, q_pos):
    L = score.shape[-1]
    k = min(TOPK_MAX, L // 4)
    allowed = jnp.arange(L, dtype=jnp.int32)[None, :] <= q_pos[:, None]
    _, sel = lax.top_k(jnp.where(allowed, score, -jnp.inf), k)
    valid = sel <= q_pos[:, None]
    return sel, valid


def dsa_attend(q, k_sel, v_sel, sel, valid, q_pos, table):
    grp = DSA_HEADS // DSA_KV_HEADS
    qg = q.reshape(q.shape[:-2] + (DSA_KV_HEADS, grp, DSA_HDIM))
    logits = jnp.einsum('...tkgd,...tskd->...tkgs', qg, k_sel).astype(jnp.float32) * DSA_SCALE
    bias = jnp.swapaxes(t5_bias(q_pos[:, None] - sel, table), -1, -2)
    bias = bias.reshape(bias.shape[:-2] + (DSA_KV_HEADS, grp, bias.shape[-1]))
    logits = jnp.where(valid[..., None, None, :], logits + bias, NEG)
    p = jax.nn.softmax(logits, axis=-1).astype(v_sel.dtype)
    o = jnp.einsum('...tkgs,...tskd->...tkgd', p, v_sel)
    return o.reshape(o.shape[:-3] + (DSA_HEADS * DSA_HDIM,))


def attn_projections(h, pos, mix_w):
    (w_in, g_cq, w_uq, gq_nope, gq_pair, g_ckv, w_ukv, gk_nope, gk_pair, dsa_gq, dsa_gk) = mix_w
    B, T = h.shape[0], h.shape[1]
    z = jnp.einsum('btd,df->btf', h, w_in)
    cuts = np.cumsum(SPLIT_SIZES)[:-1].tolist()
    cq, ckv, kr, qd, kd, vd, qi, ki, wi, ga, gb = jnp.split(z, cuts, axis=-1)
    q = jnp.einsum('btc,cf->btf', rmsnorm(cq, g_cq), w_uq).reshape(B, T, MLA_HEADS, MLA_NOPE + MLA_ROPE)
    q_mla = mla_qk_norm(q[..., :MLA_NOPE], rope(q[..., MLA_NOPE:], pos), gq_nope, gq_pair)
    ckv = rmsnorm(ckv, g_ckv)
    krope = rope(kr, pos)
    q_dsa = rmsnorm(qd.reshape(B, T, DSA_HEADS, DSA_HDIM), dsa_gq)
    k_dsa = rmsnorm(kd.reshape(B, T, DSA_KV_HEADS, DSA_HDIM), dsa_gk)
    v_dsa = vd.reshape(B, T, DSA_KV_HEADS, DSA_HDIM)
    qi = qi.reshape(B, T, IDX_HEADS, IDX_DIM)
    return q_mla, ckv, krope, q_dsa, k_dsa, v_dsa, qi, ki, wi, ga, gb


def merge_branches(oa, ob, ga, gb, merge_w):
    w_pa, w_pb, w_o = merge_w
    ya = jnp.einsum('btf,fd->btd', oa, w_pa)
    yb = jnp.einsum('btf,fd->btd', ob, w_pb)
    y = jax.nn.sigmoid(ga) * ya + jax.nn.sigmoid(gb) * yb
    return jnp.einsum('btd,de->bte', y, w_o)


def prompt_mixers(h, mix_w, merge_w, table):
    B, S = h.shape[0], h.shape[1]
    pos = jnp.arange(S, dtype=jnp.int32)
    (q_mla, ckv, krope, q_dsa, k_dsa, v_dsa, qi, ki, wi, ga, gb) = attn_projections(h, pos, mix_w)
    k_mla, v_mla = mla_keys(ckv, krope, mix_w[6], mix_w[7], mix_w[8])
    qb = min(Q_BLOCK, S)
    nblk = S // qb

    def to_blocks(a):
        return jnp.moveaxis(a.reshape((B, nblk, qb) + a.shape[2:]), 1, 0)

    def block(args):
        qm, qd, qib, wib, tq = args
        s = jnp.einsum('bthd,bshd->bhts', qm, k_mla).astype(jnp.float32) * MLA_SCALE
        s = jnp.where(pos[None, None, None, :] <= tq[None, None, :, None], s, NEG)
        p = jax.nn.softmax(s, axis=-1).astype(v_mla.dtype)
        oa = jnp.einsum('bhts,bshd->bthd', p, v_mla).reshape(B, qb, MLA_HEADS * MLA_VDIM)
        score = index_scores(qib, ki, wib)
        sel, valid = select_keys(score, tq)
        k_sel = jax.vmap(lambda kk, ss: kk[ss])(k_dsa, sel)
        v_sel = jax.vmap(lambda vv, ss: vv[ss])(v_dsa, sel)
        ob = dsa_attend(qd, k_sel, v_sel, sel, valid, tq, table)
        return oa, ob

    oa, ob = lax.map(block, (to_blocks(q_mla), to_blocks(q_dsa), to_blocks(qi), to_blocks(wi),
                             pos.reshape(nblk, qb)))
    oa = jnp.moveaxis(oa, 0, 1).reshape(B, S, MLA_HEADS * MLA_VDIM)
    ob = jnp.moveaxis(ob, 0, 1).reshape(B, S, DSA_HEADS * DSA_HDIM)
    y = merge_branches(oa, ob, ga, gb, merge_w)
    return y, (ckv, krope, k_dsa, v_dsa, ki)


def sample_mixers(h, c_ckv, c_krope, c_k, c_v, c_idxk, page_table, mix_w, merge_w, table):
    T = h.shape[1]
    past = page_table.shape[1] * PAGE_SIZE
    pos = past + jnp.arange(T, dtype=jnp.int32)
    k_pos = jnp.arange(past + T, dtype=jnp.int32)
    (q_mla, ckv, krope, q_dsa, k_dsa, v_dsa, qi, ki, wi, ga, gb) = attn_projections(h, pos, mix_w)
    ck_flat = c_k.reshape((-1, DSA_KV_HEADS, DSA_HDIM))
    cv_flat = c_v.reshape((-1, DSA_KV_HEADS, DSA_HDIM))

    def one_seq(args):
        pages, qm, ckv_n, kr_n, qd, kd_n, vd_n, qib, kib_n, wib = args
        ckv_all = jnp.concatenate([c_ckv[pages].reshape(past, MLA_KV_RANK), ckv_n], axis=0)
        kr_all = jnp.concatenate([c_krope[pages].reshape(past, MLA_ROPE), kr_n], axis=0)
        k_all, v_all = mla_keys(ckv_all, kr_all, mix_w[6], mix_w[7], mix_w[8])
        s = jnp.einsum('thd,shd->hts', qm, k_all).astype(jnp.float32) * MLA_SCALE
        s = jnp.where(k_pos[None, None, :] <= pos[None, :, None], s, NEG)
        p = jax.nn.softmax(s, axis=-1).astype(v_all.dtype)
        oa = jnp.einsum('hts,shd->thd', p, v_all).reshape(T, MLA_HEADS * MLA_VDIM)
        ki_all = jnp.concatenate([c_idxk[pages].reshape(past, IDX_DIM), kib_n], axis=0)
        score = index_scores(qib, ki_all, wib)
        sel, valid = select_keys(score, pos)
        in_past = sel < past
        ps = jnp.minimum(sel, past - 1)
        phys = pages[ps // PAGE_SIZE] * PAGE_SIZE + ps % PAGE_SIZE
        nw = jnp.clip(sel - past, 0, T - 1)
        k_sel = jnp.where(in_past[..., None, None], ck_flat[phys], kd_n[nw])
        v_sel = jnp.where(in_past[..., None, None], cv_flat[phys], vd_n[nw])
        ob = dsa_attend(qd, k_sel, v_sel, sel, valid, pos, table)
        return oa, ob

    oa, ob = lax.map(one_seq, (page_table, q_mla, ckv, krope, q_dsa, k_dsa, v_dsa, qi, ki, wi))
    y = merge_branches(oa, ob, ga, gb, merge_w)
    return y, (ckv, krope, k_dsa, v_dsa, ki)


def peer_ffn(h, wq, k1, k2, u_tab, v_tab):
    shp = h.shape
    xt = h.reshape(-1, D_MODEL)
    n = xt.shape[0]
    nb = -(-n // PEER_BLOCK)
    xt = jnp.pad(xt, ((0, nb * PEER_BLOCK - n), (0, 0)))

    def block(xb):
        q = jnp.einsum('nd,df->nf', xb, wq).reshape(-1, PEER_HEADS, 2, PEER_HALF)
        s1 = jnp.einsum('nhd,hkd->nhk', q[:, :, 0], k1).astype(jnp.float32)
        s2 = jnp.einsum('nhd,hkd->nhk', q[:, :, 1], k2).astype(jnp.float32)
        v1, i1 = lax.top_k(s1, PEER_TOPK)
        v2, i2 = lax.top_k(s2, PEER_TOPK)
        cand = (v1[..., :, None] + v2[..., None, :]).reshape(v1.shape[:-1] + (PEER_TOPK * PEER_TOPK,))
        vals, ci = lax.top_k(cand, PEER_TOPK)
        e1 = jnp.take_along_axis(i1, ci // PEER_TOPK, axis=-1)
        e2 = jnp.take_along_axis(i2, ci % PEER_TOPK, axis=-1)
        idx = e1 * PEER_KEYS + e2
        g = jax.nn.softmax(vals, axis=-1)
        a = jax.nn.gelu(jnp.einsum('nhkd,nd->nhk', u_tab[idx], xb).astype(jnp.float32), approximate=False)
        w = (g * a).astype(xb.dtype)
        return jnp.einsum('nhk,nhkd->nd', w, v_tab[idx])

    out = lax.map(block, xt.reshape(nb, PEER_BLOCK, D_MODEL))
    return out.reshape(-1, D_MODEL)[:n].reshape(shp)


def adaln(c, w, b):
    m = jnp.einsum('bd,df->bf', jax.nn.silu(c), w) + b
    return [a[:, None, :] for a in jnp.split(m, 6, axis=-1)]


def modulate(hn, shift, scale):
    return hn * (1 + scale) + shift


def stack_layers(states):
    return tuple(jnp.stack([s[i] for s in states]) for i in range(len(states[0])))


def setup_inputs(seed: int = 0) -> dict:
    key = jax.random.key(seed)
    ks = jax.random.split(key, 40)
    n_pages = PAST_LEN // PAGE_SIZE
    used = DEC_BATCH * n_pages
    n_pool = used + max(1, used // 4)

    def nrm(k, shape, scale=1.0):
        return jax.random.normal(k, shape, jnp.float32) * scale

    def gain(k, shape):
        return 1.0 + 0.1 * jax.random.normal(k, shape, jnp.float32)

    page_table = jax.random.permutation(ks[0], n_pool)[:used].reshape(DEC_BATCH, n_pages).astype(jnp.int32)
    return {
        'x_prompt': nrm(ks[1], (BATCH, SEQ, D_MODEL)),
        'x_sample': nrm(ks[2], (DEC_BATCH, DEC_SEQ, D_MODEL)),
        'c_prompt': nrm(ks[3], (BATCH, D_MODEL)),
        'c_sample': nrm(ks[4], (DEC_BATCH, D_MODEL)),
        'cache_ckv': nrm(ks[5], (DEPTH, n_pool, PAGE_SIZE, MLA_KV_RANK)),
        'cache_krope': nrm(ks[6], (DEPTH, n_pool, PAGE_SIZE, MLA_ROPE)),
        'cache_k': nrm(ks[7], (DEPTH, n_pool, PAGE_SIZE, DSA_KV_HEADS, DSA_HDIM)),
        'cache_v': nrm(ks[8], (DEPTH, n_pool, PAGE_SIZE, DSA_KV_HEADS, DSA_HDIM)),
        'cache_idx_k': nrm(ks[9], (DEPTH, n_pool, PAGE_SIZE, IDX_DIM)),
        'page_table': page_table,
        'w_in': nrm(ks[10], (DEPTH, D_MODEL, N_IN), D_MODEL ** -0.5),
        'g_cq': gain(ks[11], (DEPTH, MLA_Q_RANK)),
        'w_uq': nrm(ks[12], (DEPTH, MLA_Q_RANK, MLA_HEADS * (MLA_NOPE + MLA_ROPE)), MLA_Q_RANK ** -0.5),
        'mla_gq_nope': gain(ks[13], (DEPTH, MLA_NOPE)),
        'mla_gq_pair': gain(ks[14], (DEPTH, MLA_ROPE // 2)),
        'g_ckv': gain(ks[15], (DEPTH, MLA_KV_RANK)),
        'w_ukv': nrm(ks[16], (DEPTH, MLA_KV_RANK, MLA_HEADS * (MLA_NOPE + MLA_VDIM)), MLA_KV_RANK ** -0.5),
        'mla_gk_nope': gain(ks[17], (DEPTH, MLA_NOPE)),
        'mla_gk_pair': gain(ks[18], (DEPTH, MLA_ROPE // 2)),
        'dsa_gq': gain(ks[19], (DEPTH, DSA_HDIM)),
        'dsa_gk': gain(ks[20], (DEPTH, DSA_HDIM)),
        't5_table': nrm(ks[21], (N_BUCKETS, DSA_HEADS), 0.5),
        'w_proj_a': nrm(ks[22], (DEPTH, MLA_HEADS * MLA_VDIM, D_MODEL), (MLA_HEADS * MLA_VDIM) ** -0.5),
        'w_proj_b': nrm(ks[23], (DEPTH, DSA_HEADS * DSA_HDIM, D_MODEL), (DSA_HEADS * DSA_HDIM) ** -0.5),
        'w_out': nrm(ks[24], (DEPTH, D_MODEL, D_MODEL), D_MODEL ** -0.5),
        'g_attn': gain(ks[25], (DEPTH, D_MODEL)),
        'g_ffn': gain(ks[26], (DEPTH, D_MODEL)),
        'ada_w': nrm(ks[27], (DEPTH, D_MODEL, 6 * D_MODEL), 0.5 * D_MODEL ** -0.5),
        'ada_b': nrm(ks[28], (DEPTH, 6 * D_MODEL), 0.02),
        'peer_wq': nrm(ks[29], (DEPTH, D_MODEL, PEER_HEADS * PEER_QDIM), D_MODEL ** -0.5),
        'peer_k1': nrm(ks[30], (DEPTH, PEER_HEADS, PEER_KEYS, PEER_HALF), PEER_HALF ** -0.5),
        'peer_k2': nrm(ks[31], (DEPTH, PEER_HEADS, PEER_KEYS, PEER_HALF), PEER_HALF ** -0.5),
        'peer_u': nrm(ks[32], (DEPTH, PEER_EXPERTS, D_MODEL), D_MODEL ** -0.5),
        'peer_v': nrm(ks[33], (DEPTH, PEER_EXPERTS, D_MODEL), 0.3),
    }


def reference(x_prompt, x_sample, c_prompt, c_sample, cache_ckv, cache_krope, cache_k, cache_v,
              cache_idx_k, page_table, w_in, g_cq, w_uq, mla_gq_nope, mla_gq_pair, g_ckv, w_ukv,
              mla_gk_nope, mla_gk_pair, dsa_gq, dsa_gk, t5_table, w_proj_a, w_proj_b, w_out,
              g_attn, g_ffn, ada_w, ada_b, peer_wq, peer_k1, peer_k2, peer_u, peer_v):
    xp, xs = x_prompt, x_sample
    st_p, st_s = [], []
    for l in range(DEPTH):
        mix_w = (w_in[l], g_cq[l], w_uq[l], mla_gq_nope[l], mla_gq_pair[l], g_ckv[l], w_ukv[l],
                 mla_gk_nope[l], mla_gk_pair[l], dsa_gq[l], dsa_gk[l])
        merge_w = (w_proj_a[l], w_proj_b[l], w_out[l])
        peer_w = (peer_wq[l], peer_k1[l], peer_k2[l], peer_u[l], peer_v[l])
        mp = adaln(c_prompt, ada_w[l], ada_b[l])
        ms = adaln(c_sample, ada_w[l], ada_b[l])
        ap, sp = prompt_mixers(modulate(rmsnorm(xp, g_attn[l]), mp[0], mp[1]), mix_w, merge_w, t5_table)
        a_s, ss = sample_mixers(modulate(rmsnorm(xs, g_attn[l]), ms[0], ms[1]), cache_ckv[l], cache_krope[l],
                                cache_k[l], cache_v[l], cache_idx_k[l], page_table, mix_w, merge_w, t5_table)
        xp = xp + mp[2] * ap
        xs = xs + ms[2] * a_s
        xp = xp + mp[5] * peer_ffn(modulate(rmsnorm(xp, g_ffn[l]), mp[3], mp[4]), *peer_w)
        xs = xs + ms[5] * peer_ffn(modulate(rmsnorm(xs, g_ffn[l]), ms[3], ms[4]), *peer_w)
        st_p.append(sp)
        st_s.append(ss)
    ckv_p, krope_p, k_p, v_p, idxk_p = stack_layers(st_p)
    ckv_s, krope_s, k_s, v_s, idxk_s = stack_layers(st_s)
    return (xp, xs, ckv_p, krope_p, k_p, v_p, idxk_p, ckv_s, krope_s, k_s, v_s, idxk_s)
```

```python
import functools
import math

import jax
import jax.numpy as jnp
from jax import lax
from jax.experimental import pallas as pl
from jax.experimental.pallas import tpu as pltpu

F32 = jnp.float32
BF16 = jnp.bfloat16

MLA_NOPE = 128
MLA_VDIM = 128
TOPK_MAX = 256
MAX_DISTANCE = 128
PEER_TOPK = 16
ROPE_THETA = 10000.0
EPS = 1e-6
NEG = -1e30

LANE = 128
VMEM_LIMIT = 56 * 1024 * 1024
PAGES_PER_STEP = 8


def _nt(a, b):
    return lax.dot_general(a, b, (((1,), (1,)), ((), ())), preferred_element_type=F32)


def _nn(a, b):
    return jnp.dot(a, b, preferred_element_type=F32)


def _params(*sem):
    return pltpu.CompilerParams(dimension_semantics=sem, vmem_limit_bytes=VMEM_LIMIT)


def _tile(m, pref):
    if m <= pref:
        return m
    for t in range(pref, 7, -1):
        if m % t == 0 and t % 8 == 0:
            return t
    return m


def _mod_spec(a, tm, rows_per_seq):
    d = a.shape[-1]
    if a.ndim == 3:
        bps = rows_per_seq // tm
        return pl.BlockSpec((None, 1, d), lambda i, *_: (i // bps, 0, 0))
    return pl.BlockSpec((tm, d), lambda i, *_: (i, 0))


def _mm_body(*refs, pre_silu, has_bias):
    if has_bias:
        x_ref, w_ref, b_ref, o_ref = refs
    else:
        x_ref, w_ref, o_ref = refs
    x = x_ref[...]
    if pre_silu:
        x = x * jax.nn.sigmoid(x)
    acc = _nn(x.astype(BF16), w_ref[...].astype(BF16))
    if has_bias:
        acc = acc + b_ref[...]
    o_ref[...] = acc.astype(o_ref.dtype)


def _matmul(x, w, *, tm, tn, out_dtype, bias=None, pre_silu=False, name="mm"):
    m, k = x.shape
    n = w.shape[1]
    in_specs = [pl.BlockSpec((tm, k), lambda i, j: (i, 0)),
                pl.BlockSpec((k, tn), lambda i, j: (0, j))]
    args = [x, w]
    if bias is not None:
        in_specs.append(pl.BlockSpec((1, tn), lambda i, j: (0, j)))
        args.append(bias)
    return pl.pallas_call(
        functools.partial(_mm_body, pre_silu=pre_silu, has_bias=bias is not None),
        grid=(m // tm, n // tn),
        in_specs=in_specs,
        out_specs=pl.BlockSpec((tm, tn), lambda i, j: (i, j)),
        out_shape=jax.ShapeDtypeStruct((m, n), out_dtype),
        compiler_params=_params("parallel", "arbitrary"),
        name=name,
    )(*args)


def _norm_mod_body(*refs, residual):
    if residual:
        x_ref, y_ref, gt_ref, g_ref, sc_ref, sh_ref, x1_ref, h_ref = refs
        x = x_ref[...] + gt_ref[...] * y_ref[...]
        x1_ref[...] = x
    else:
        x_ref, g_ref, sc_ref, sh_ref, h_ref = refs
        x = x_ref[...]
    ms = jnp.mean(x * x, axis=-1, keepdims=True)
    y = x * lax.rsqrt(ms + EPS) * g_ref[...]
    h_ref[...] = (y * (1.0 + sc_ref[...]) + sh_ref[...]).astype(h_ref.dtype)


def _norm_mod(x, g, scale, shift, *, rows_per_seq, y=None, gate=None):
    m, d = x.shape
    tm = _tile(rows_per_seq, 256) if scale.ndim == 3 else _tile(m, 128)
    row = pl.BlockSpec((tm, d), lambda i: (i, 0))
    vec = pl.BlockSpec((1, d), lambda i: (0, 0))
    residual = y is not None
    if residual:
        in_specs = [row, row, _mod_spec(gate, tm, rows_per_seq), vec,
                    _mod_spec(scale, tm, rows_per_seq), _mod_spec(shift, tm, rows_per_seq)]
        args = [x, y, gate, g.reshape(1, d), scale, shift]
        out_specs = [row, row]
        out_shape = [jax.ShapeDtypeStruct((m, d), F32), jax.ShapeDtypeStruct((m, d), BF16)]
    else:
        in_specs = [row, vec, _mod_spec(scale, tm, rows_per_seq), _mod_spec(shift, tm, rows_per_seq)]
        args = [x, g.reshape(1, d), scale, shift]
        out_specs = row
        out_shape = jax.ShapeDtypeStruct((m, d), BF16)
    return pl.pallas_call(
        functools.partial(_norm_mod_body, residual=residual),
        grid=(m // tm,), in_specs=in_specs, out_specs=out_specs, out_shape=out_shape,
        compiler_params=_params("parallel"), name="norm_mod",
    )(*args)


def _residual_t_body(x_ref, yt_ref, gt_ref, o_ref):
    o_ref[...] = x_ref[...] + gt_ref[...] * yt_ref[...].T


def _residual_t(x, yt, gate, *, rows_per_seq):
    m, d = x.shape
    tm = _tile(rows_per_seq if gate.ndim == 3 else m, 256)
    row = pl.BlockSpec((tm, d), lambda i: (i, 0))
    return pl.pallas_call(
        _residual_t_body, grid=(m // tm,),
        in_specs=[row, pl.BlockSpec((d, tm), lambda i: (0, i)), _mod_spec(gate, tm, rows_per_seq)],
        out_specs=row, out_shape=jax.ShapeDtypeStruct((m, d), F32),
        compiler_params=_params("parallel"), name="residual",
    )(x, yt, gate)


def _rope_tables(pos, half):
    inv = jnp.power(ROPE_THETA, -jnp.arange(half, dtype=F32) / half)
    ang = pos.astype(F32)[:, None] * inv[None, :]
    cos, sin = jnp.cos(ang), jnp.sin(ang)
    t = pos.shape[0]
    z = lambda w: jnp.zeros((t, w), F32)
    cosv = jnp.concatenate([cos, cos, z(LANE - 2 * half)], axis=1)
    sin_a = jnp.concatenate([z(half), sin, z(LANE - 2 * half)], axis=1)
    sin_b = jnp.concatenate([-sin, z(LANE - half)], axis=1)
    return cosv, sin_a, sin_b


def _rope_slot(v, cosv, sin_a, sin_b, half):
    return v * cosv + pltpu.roll(v, half, 1) * sin_a + pltpu.roll(v, LANE - half, 1) * sin_b


def _mla_q_body(cq_ref, gcq_ref, wuq_ref, cos_ref, sa_ref, sb_ref, gn_ref, gr_ref, q_ref, *, heads, half):
    c = cq_ref[...]
    cn = c * lax.rsqrt(jnp.mean(c * c, axis=-1, keepdims=True) + EPS) * gcq_ref[...]
    q = _nn(cn.astype(BF16), wuq_ref[...])
    cosv, sin_a, sin_b = cos_ref[...], sa_ref[...], sb_ref[...]
    inv_dim = 1.0 / (MLA_NOPE + 2 * half)
    for h in range(heads):
        nope = q[:, 2 * LANE * h:2 * LANE * h + LANE]
        rot = _rope_slot(q[:, 2 * LANE * h + LANE:2 * LANE * (h + 1)], cosv, sin_a, sin_b, half)
        ss = jnp.sum(nope * nope + rot * rot, axis=-1, keepdims=True)
        r = lax.rsqrt(ss * inv_dim + EPS)
        q_ref[:, 2 * LANE * h:2 * LANE * h + LANE] = (nope * r * gn_ref[...]).astype(BF16)
        q_ref[:, 2 * LANE * h + LANE:2 * LANE * (h + 1)] = (rot * r * gr_ref[...]).astype(BF16)


def _mla_q(z, zoff, g_cq, wuq_p, tabs, gn, gr, *, heads, half, rows_per_seq):
    m = z.shape[0]
    rq = g_cq.shape[-1]
    tm = _tile(rows_per_seq, 256)
    tab = pl.BlockSpec((tm, LANE), lambda i: (i % (rows_per_seq // tm), 0))
    vec = lambda w: pl.BlockSpec((1, w), lambda i: (0, 0))
    return pl.pallas_call(
        functools.partial(_mla_q_body, heads=heads, half=half),
        grid=(m // tm,),
        in_specs=[pl.BlockSpec((tm, rq), lambda i: (i, zoff["cq"] // rq)), vec(rq),
                  pl.BlockSpec(wuq_p.shape, lambda i: (0, 0)), tab, tab, tab, vec(LANE), vec(LANE)],
        out_specs=pl.BlockSpec((tm, heads * 2 * LANE), lambda i: (i, 0)),
        out_shape=jax.ShapeDtypeStruct((m, heads * 2 * LANE), BF16),
        compiler_params=_params("parallel"), name="mla_q",
    )(z, g_cq.reshape(1, rq), wuq_p, *tabs, gn, gr)


def _post_body(*refs, heads, half, kvh, dh, dsa_heads, want_kv):
    (ckv_ref, kr_ref, kd_ref, qd_ref, gckv_ref, cos_ref, sa_ref, sb_ref, gq_ref, gk_ref) = refs[:10]
    if want_kv:
        wuk_ref, wuv_ref = refs[10:12]
        ckvn_ref, krope_ref, kdn_ref, qdn_ref, kf_ref, vm_ref = refs[12:]
    else:
        ckvn_ref, krope_ref, kdn_ref, qdn_ref = refs[10:]
    c = ckv_ref[...]
    cn = c * lax.rsqrt(jnp.mean(c * c, axis=-1, keepdims=True) + EPS) * gckv_ref[...]
    ckvn_ref[...] = cn
    rot = _rope_slot(kr_ref[...], cos_ref[...], sa_ref[...], sb_ref[...], half)
    krope_ref[...] = rot[:, :2 * half]
    kd = kd_ref[...]
    for j in range(kvh):
        x = kd[:, dh * j:dh * (j + 1)]
        kdn_ref[:, dh * j:dh * (j + 1)] = x * lax.rsqrt(jnp.mean(x * x, axis=-1, keepdims=True) + EPS) * gk_ref[...]
    qd = qd_ref[...]
    for h in range(dsa_heads):
        x = qd[:, dh * h:dh * (h + 1)]
        n = x * lax.rsqrt(jnp.mean(x * x, axis=-1, keepdims=True) + EPS) * gq_ref[...]
        qdn_ref[:, dh * h:dh * (h + 1)] = (n * (dh ** -0.5)).astype(BF16)
    if want_kv:
        cb = cn.astype(BF16)
        kn = _nn(cb, wuk_ref[...])
        vm_ref[...] = _nn(cb, wuv_ref[...]).astype(BF16)
        rs = jnp.sum(rot * rot, axis=-1, keepdims=True)
        inv_dim = 1.0 / (MLA_NOPE + 2 * half)
        for h in range(heads):
            x = kn[:, LANE * h:LANE * (h + 1)]
            r = lax.rsqrt((jnp.sum(x * x, axis=-1, keepdims=True) + rs) * inv_dim + EPS)
            kf_ref[:, 2 * LANE * h:2 * LANE * h + LANE] = (x * r).astype(BF16)
            kf_ref[:, 2 * LANE * h + LANE:2 * LANE * (h + 1)] = (rot * r).astype(BF16)


def _post_proj(z, zoff, g_ckv, tabs, dsa_gq, dsa_gk, wuk, wuv, *, heads, half, kvh, dh, dsa_heads,
               rows_per_seq, want_kv):
    m = z.shape[0]
    rk = g_ckv.shape[-1]
    tm = _tile(rows_per_seq, 256)
    tab = pl.BlockSpec((tm, LANE), lambda i: (i % (rows_per_seq // tm), 0))
    vec = lambda w: pl.BlockSpec((1, w), lambda i: (0, 0))
    col = lambda name, w: pl.BlockSpec((tm, w), lambda i: (i, zoff[name] // w))
    full = lambda w: pl.BlockSpec((tm, w), lambda i: (i, 0))
    in_specs = [col("ckv", rk), col("kr", LANE), col("kd", kvh * dh), col("qd", dsa_heads * dh),
                vec(rk), tab, tab, tab, vec(dh), vec(dh)]
    args = [z, z, z, z, g_ckv.reshape(1, rk), *tabs, dsa_gq.reshape(1, dh), dsa_gk.reshape(1, dh)]
    out_specs = [full(rk), full(2 * half), full(kvh * dh), full(dsa_heads * dh)]
    out_shape = [jax.ShapeDtypeStruct((m, rk), F32), jax.ShapeDtypeStruct((m, 2 * half), F32),
                 jax.ShapeDtypeStruct((m, kvh * dh), F32), jax.ShapeDtypeStruct((m, dsa_heads * dh), BF16)]
    if want_kv:
        in_specs += [pl.BlockSpec(wuk.shape, lambda i: (0, 0)), pl.BlockSpec(wuv.shape, lambda i: (0, 0))]
        args += [wuk, wuv]
        out_specs += [full(heads * 2 * LANE), full(heads * MLA_VDIM)]
        out_shape += [jax.ShapeDtypeStruct((m, heads * 2 * LANE), BF16),
                      jax.ShapeDtypeStruct((m, heads * MLA_VDIM), BF16)]
    return pl.pallas_call(
        functools.partial(_post_body, heads=heads, half=half, kvh=kvh, dh=dh, dsa_heads=dsa_heads,
                          want_kv=want_kv),
        grid=(m // tm,), in_specs=in_specs, out_specs=out_specs, out_shape=out_shape,
        compiler_params=_params("parallel"), name="post_proj",
    )(*args)


def _mla_prompt_body(q_ref, k_ref, v_ref, o_ref, *, tq):
    i = pl.program_id(2)
    s = _nt(q_ref[...], k_ref[...])
    row = i * tq + lax.broadcasted_iota(jnp.int32, s.shape, 0)
    colv = lax.broadcasted_iota(jnp.int32, s.shape, 1)
    s = jnp.where(colv <= row, s, NEG)
    mx = jnp.max(s, axis=-1, keepdims=True)
    p = jnp.exp(s - mx)
    l = jnp.sum(p, axis=-1, keepdims=True)
    o = _nn(p.astype(BF16), v_ref[...])
    o_ref[...] = (o / l).astype(o_ref.dtype)


def _mla_prompt_attn(qf, kf, vm, *, batch, seq, heads):
    tq = _tile(seq, 256)
    nq = seq // tq
    return pl.pallas_call(
        functools.partial(_mla_prompt_body, tq=tq),
        grid=(batch, heads, nq),
        in_specs=[pl.BlockSpec((tq, 2 * LANE), lambda b, h, i: (b * nq + i, h)),
                  pl.BlockSpec((seq, 2 * LANE), lambda b, h, i: (b, h)),
                  pl.BlockSpec((seq, MLA_VDIM), lambda b, h, i: (b, h))],
        out_specs=pl.BlockSpec((tq, MLA_VDIM), lambda b, h, i: (b * nq + i, h)),
        out_shape=jax.ShapeDtypeStruct((batch * seq, heads * MLA_VDIM), BF16),
        compiler_params=_params("parallel", "parallel", "arbitrary"), name="mla_prompt_attn",
    )(qf, kf, vm)


def _index_prompt_body(qi_ref, ki_ref, wi_ref, o_ref, *, idx_heads, di, tq, cscale):
    i = pl.program_id(1)
    kb = ki_ref[...].astype(BF16)
    w = wi_ref[...] * cscale
    acc = jnp.zeros(o_ref.shape, F32)
    for h in range(idx_heads):
        s = _nt(qi_ref[:, di * h:di * (h + 1)].astype(BF16), kb)
        acc = acc + jnp.maximum(s, 0.0) * w[:, h:h + 1]
    row = i * tq + lax.broadcasted_iota(jnp.int32, acc.shape, 0)
    colv = lax.broadcasted_iota(jnp.int32, acc.shape, 1)
    o_ref[...] = jnp.where(colv <= row, acc, -jnp.inf)


def _index_prompt(z, zoff, *, batch, seq, idx_heads, di):
    tq = _tile(seq, 256)
    nq = seq // tq
    wq = idx_heads * di
    cscale = (di ** -0.5) * (idx_heads ** -0.5)
    return pl.pallas_call(
        functools.partial(_index_prompt_body, idx_heads=idx_heads, di=di, tq=tq, cscale=cscale),
        grid=(batch, nq),
        in_specs=[pl.BlockSpec((tq, wq), lambda b, i: (b * nq + i, zoff["qi"] // wq)),
                  pl.BlockSpec((seq, di), lambda b, i: (b, zoff["ki"] // di)),
                  pl.BlockSpec((tq, LANE), lambda b, i: (b * nq + i, zoff["wi"] // LANE))],
        out_specs=pl.BlockSpec((tq, seq), lambda b, i: (b * nq + i, 0)),
        out_shape=jax.ShapeDtypeStruct((batch * seq, seq), F32),
        compiler_params=_params("parallel", "arbitrary"), name="index_prompt",
    )(z, z, z)


def _index_sample_body(pt_ref, *refs, npg, page, nch, tok, idx_heads):
    pages = refs[:npg]
    qi_ref, w_ref, kn_ref, o_ref = refs[npg:]
    c = pl.program_id(1)
    q = qi_ref[...]
    w = w_ref[...]

    def scores(kc):
        s = jnp.maximum(_nt(q, kc.astype(BF16)), 0.0) * w
        return jnp.sum(s.reshape(tok, idx_heads, s.shape[-1]), axis=1)

    @pl.when(c < nch)
    def _():
        for i in range(npg):
            o_ref[:, page * i:page * (i + 1)] = scores(pages[i][...])

    @pl.when(c == nch)
    def _():
        s = scores(kn_ref[...])
        row = lax.broadcasted_iota(jnp.int32, s.shape, 0)
        colv = lax.broadcasted_iota(jnp.int32, s.shape, 1)
        o_ref[:, :page] = jnp.where(colv <= row, s, -jnp.inf)
        o_ref[:, page:] = jnp.full((tok, (npg - 1) * page), -jnp.inf, F32)


def _index_sample(cache_idx, layer, page_table, qi_s, wi_s, ki_new, *, tok, idx_heads):
    nseq, npages = page_table.shape
    page, di = cache_idx.shape[2], cache_idx.shape[3]
    npg = PAGES_PER_STEP
    nch = npages // npg
    chunk = npg * page
    lpad = (nch + 1) * chunk
    page_specs = [pl.BlockSpec((None, None, page, di),
                               lambda b, c, pt, i=i: (layer, pt[b, jnp.minimum(c, nch - 1) * npg + i], 0, 0))
                  for i in range(npg)]
    grid_spec = pltpu.PrefetchScalarGridSpec(
        num_scalar_prefetch=1, grid=(nseq, nch + 1),
        in_specs=page_specs + [
            pl.BlockSpec((None, tok * idx_heads, di), lambda b, c, pt: (b, 0, 0)),
            pl.BlockSpec((None, tok * idx_heads, 1), lambda b, c, pt: (b, 0, 0)),
            pl.BlockSpec((None, page, di), lambda b, c, pt: (b, 0, 0))],
        out_specs=pl.BlockSpec((None, tok, chunk), lambda b, c, pt: (b, 0, c)))
    return pl.pallas_call(
        functools.partial(_index_sample_body, npg=npg, page=page, nch=nch, tok=tok, idx_heads=idx_heads),
        grid_spec=grid_spec,
        out_shape=jax.ShapeDtypeStruct((nseq, tok, lpad), F32),
        compiler_params=_params("parallel", "arbitrary"), name="index_sample",
    )(page_table, *([cache_idx] * npg), qi_s, wi_s, ki_new)


def _topk_mask_body(s_ref, o_ref, key_sc, *, k, tk):
    bits = pltpu.bitcast(s_ref[...], jnp.int32)
    key_sc[...] = jnp.where(bits < 0, bits ^ jnp.int32(0x7FFFFFFF), bits)
    rows, width = s_ref.shape

    def body(it, t):
        cand = t + jnp.left_shift(jnp.int32(1), 31 - it)
        cnt = jnp.sum(jnp.where(key_sc[...] >= cand, 1.0, 0.0), axis=-1, keepdims=True)
        return jnp.where(cnt >= float(k), cand, t)

    thr = lax.fori_loop(0, 32, body, jnp.full((rows, 1), -2 ** 31, jnp.int32))
    sel = jnp.logical_and(key_sc[...] >= thr, s_ref[...] > -jnp.inf)
    m = jnp.where(sel, 1.0, 0.0).astype(o_ref.dtype)
    if tk is None:
        o_ref[...] = m
    else:
        for j in range(width // tk):
            o_ref[j] = m[:, tk * j:tk * (j + 1)]


def _topk_mask(scores, k, *, tr, tiled):
    r, l = scores.shape
    if tiled is None:
        out_specs = pl.BlockSpec((tr, l), lambda i: (i, 0))
        out_shape = jax.ShapeDtypeStruct((r, l), BF16)
    else:
        out_specs = pl.BlockSpec((None, l // tiled, tr, tiled), lambda i: (i, 0, 0, 0))
        out_shape = jax.ShapeDtypeStruct((r // tr, l // tiled, tr, tiled), BF16)
    return pl.pallas_call(
        functools.partial(_topk_mask_body, k=k, tk=tiled),
        grid=(r // tr,),
        in_specs=[pl.BlockSpec((tr, l), lambda i: (i, 0))],
        out_specs=out_specs, out_shape=out_shape,
        scratch_shapes=[pltpu.VMEM((tr, l), jnp.int32)],
        compiler_params=_params("parallel"), name="topk_mask",
    )(scores)


def _dsa_prompt_body(q_ref, k_ref, v_ref, mask_ref, bias_ref, o_ref, m_sc, l_sc, acc_sc, *, tq, grp, dh):
    i = pl.program_id(2)
    q = jnp.concatenate([q_ref[:, dh * j:dh * (j + 1)] for j in range(grp)], axis=0)
    m_sc[...] = jnp.full(m_sc.shape, NEG, F32)
    l_sc[...] = jnp.zeros(l_sc.shape, F32)
    acc_sc[...] = jnp.zeros(acc_sc.shape, F32)

    def step(kt, bias):
        kb = k_ref[pl.ds(pl.multiple_of(kt * tq, tq), tq), :].astype(BF16)
        vb = v_ref[pl.ds(pl.multiple_of(kt * tq, tq), tq), :].astype(BF16)
        s = _nt(q, kb)
        if bias is not None:
            s = s + bias
        mk = mask_ref[kt]
        keep = jnp.concatenate([mk] * grp, axis=0) > 0
        s = jnp.where(keep, s, NEG)
        m_old = m_sc[...]
        m_new = jnp.maximum(m_old, jnp.max(s, axis=-1, keepdims=True))
        alpha = jnp.exp(m_old - m_new)
        p = jnp.exp(s - m_new)
        l_sc[...] = alpha * l_sc[...] + jnp.sum(p, axis=-1, keepdims=True)
        acc_sc[...] = alpha * acc_sc[...] + _nn(p.astype(BF16), vb)
        m_sc[...] = m_new

    def far(kt, carry):
        step(kt, None)
        return carry

    lax.fori_loop(0, jnp.maximum(i - 1, 0), far, 0)

    @pl.when(i >= 1)
    def _():
        step(i - 1, bias_ref[:, 1].reshape(grp * tq, tq))

    step(i, bias_ref[:, 0].reshape(grp * tq, tq))
    o = acc_sc[...] / l_sc[...]
    for j in range(grp):
        o_ref[:, dh * j:dh * (j + 1)] = o[tq * j:tq * (j + 1)].astype(o_ref.dtype)


def _dsa_prompt_attn(qdn, kdn, z, zoff, mask3, bias_tiles, *, batch, seq, kvh, grp, dh, tq):
    nq = seq // tq
    return pl.pallas_call(
        functools.partial(_dsa_prompt_body, tq=tq, grp=grp, dh=dh),
        grid=(batch, kvh, nq),
        in_specs=[pl.BlockSpec((tq, grp * dh), lambda b, g, i: (b * nq + i, g)),
                  pl.BlockSpec((seq, dh), lambda b, g, i: (b, g)),
                  pl.BlockSpec((seq, dh), lambda b, g, i: (b, zoff["vd"] // dh + g)),
                  pl.BlockSpec((None, nq, tq, tq), lambda b, g, i: (b * nq + i, 0, 0, 0)),
                  pl.BlockSpec((grp, 2, tq, tq), lambda b, g, i: (g, 0, 0, 0))],
        out_specs=pl.BlockSpec((tq, grp * dh), lambda b, g, i: (b * nq + i, g)),
        out_shape=jax.ShapeDtypeStruct((batch * seq, kvh * grp * dh), BF16),
        scratch_shapes=[pltpu.VMEM((grp * tq, 1), F32), pltpu.VMEM((grp * tq, 1), F32),
                        pltpu.VMEM((grp * tq, dh), F32)],
        compiler_params=_params("parallel", "parallel", "arbitrary"), name="dsa_prompt_attn",
    )(qdn, kdn, z, mask3, bias_tiles)


def _mla_sample_body(pt_ref, *refs, npg, page, nch, heads, tok, rk, half):
    ckv_pages = refs[:npg]
    kr_pages = refs[npg:2 * npg]
    (qexp_ref, qrot_ref, wukt_ref, wuv_ref, cn_ref, rn_ref, o_ref, a_sc, m_sc, l_sc, acc_sc) = refs[2 * npg:]
    b, c = pl.program_id(0), pl.program_id(1)
    nrow = tok * heads
    hn = heads * MLA_NOPE
    inv_dim = 1.0 / (MLA_NOPE + 2 * half)

    @pl.when(jnp.logical_and(b == 0, c == 0))
    def _():
        a_sc[0:hn, :] = wukt_ref[...]

    @pl.when(c == 0)
    def _():
        a_sc[hn:hn + nrow, :] = _nn(qexp_ref[...], a_sc[0:hn, :]).astype(BF16)
        m_sc[...] = jnp.full(m_sc.shape, NEG, F32)
        l_sc[...] = jnp.zeros(l_sc.shape, F32)
        acc_sc[...] = jnp.zeros(acc_sc.shape, F32)

    ones = jnp.ones((8, 2 * half), BF16)

    def attend(cc, rr, keep):
        cb = cc.astype(BF16)
        n = cc.shape[0]
        kt = _nt(a_sc[...], cb)
        kn = kt[0:hn]
        kn2 = jnp.sum((kn * kn).reshape(heads, MLA_NOPE, n), axis=1)
        rs = _nt(ones, (rr * rr).astype(BF16))[0:1]
        r = lax.rsqrt((kn2 + rs) * inv_dim + EPS)
        s = kt[hn:hn + nrow] + _nt(qrot_ref[...], rr.astype(BF16))
        s = s * jnp.concatenate([r] * tok, axis=0)
        if keep is not None:
            s = jnp.where(keep, s, NEG)
        m_old = m_sc[...]
        m_new = jnp.maximum(m_old, jnp.max(s, axis=-1, keepdims=True))
        alpha = jnp.exp(m_old - m_new)
        p = jnp.exp(s - m_new)
        l_sc[...] = alpha * l_sc[...] + jnp.sum(p, axis=-1, keepdims=True)
        acc_sc[...] = alpha * acc_sc[...] + _nn(p.astype(BF16), cb)
        m_sc[...] = m_new

    cc = jnp.concatenate([p_[...] for p_ in ckv_pages], axis=0)
    rr = jnp.concatenate([p_[...] for p_ in kr_pages], axis=0)
    attend(cc, rr, None)

    @pl.when(c == nch - 1)
    def _():
        row = lax.broadcasted_iota(jnp.int32, (nrow, page), 0)
        colv = lax.broadcasted_iota(jnp.int32, (nrow, page), 1)
        attend(cn_ref[...], rn_ref[...], colv * heads <= row)
        olat = (acc_sc[...] / l_sc[...]).astype(BF16)
        full = _nn(olat, wuv_ref[...])
        rowh = lax.broadcasted_iota(jnp.int32, full.shape, 0) % heads
        colh = lax.broadcasted_iota(jnp.int32, full.shape, 1) // MLA_VDIM
        full = jnp.where(rowh == colh, full, 0.0)
        o_ref[...] = jnp.sum(full.reshape(tok, heads, heads * MLA_VDIM), axis=1).astype(o_ref.dtype)


def _mla_sample_attn(cache_ckv, cache_krope, layer, page_table, qexp, qrot, wukt, wuv, ckv_new, kr_new,
                     *, heads, tok, half):
    nseq, npages = page_table.shape
    page, rk = cache_ckv.shape[2], cache_ckv.shape[3]
    npg = PAGES_PER_STEP
    nch = npages // npg
    nrow = tok * heads
    hn = heads * MLA_NOPE
    cspec = [pl.BlockSpec((None, None, page, rk), lambda b, c, pt, i=i: (layer, pt[b, c * npg + i], 0, 0))
             for i in range(npg)]
    rspec = [pl.BlockSpec((None, None, page, 2 * half), lambda b, c, pt, i=i: (layer, pt[b, c * npg + i], 0, 0))
             for i in range(npg)]
    per_seq = lambda s1, s2: pl.BlockSpec((None, s1, s2), lambda b, c, pt: (b, 0, 0))
    const = lambda shp: pl.BlockSpec(shp, lambda b, c, pt: (0, 0))
    grid_spec = pltpu.PrefetchScalarGridSpec(
        num_scalar_prefetch=1, grid=(nseq, nch),
        in_specs=cspec + rspec + [per_seq(nrow, hn), per_seq(nrow, 2 * half), const(wukt.shape),
                                  const(wuv.shape), per_seq(page, rk), per_seq(page, 2 * half)],
        out_specs=pl.BlockSpec((None, tok, heads * MLA_VDIM), lambda b, c, pt: (b, 0, 0)),
        scratch_shapes=[pltpu.VMEM((hn + nrow, rk), BF16), pltpu.VMEM((nrow, 1), F32),
                        pltpu.VMEM((nrow, 1), F32), pltpu.VMEM((nrow, rk), F32)])
    return pl.pallas_call(
        functools.partial(_mla_sample_body, npg=npg, page=page, nch=nch, heads=heads, tok=tok, rk=rk, half=half),
        grid_spec=grid_spec,
        out_shape=jax.ShapeDtypeStruct((nseq, tok, heads * MLA_VDIM), BF16),
        compiler_params=_params("arbitrary", "arbitrary"), name="mla_sample_attn",
    )(page_table, *([cache_ckv] * npg), *([cache_krope] * npg), qexp, qrot, wukt, wuv, ckv_new, kr_new)


def _dsa_sample_body(pt_ref, *refs, npg, page, nch, kvh, dh, nrow):
    k_pages = refs[:npg]
    v_pages = refs[npg:2 * npg]
    (q_ref, mask_ref, maskn_ref, bl_ref, bn_ref, exp_ref, kn_ref, vn_ref, o_ref, m_sc, l_sc, acc_sc) = refs[2 * npg:]
    c = pl.program_id(1)

    @pl.when(c == 0)
    def _():
        m_sc[...] = jnp.full(m_sc.shape, NEG, F32)
        l_sc[...] = jnp.zeros(l_sc.shape, F32)
        acc_sc[...] = jnp.zeros(acc_sc.shape, F32)

    def attend(kk, vv, mask8, bias):
        keep = _nn(exp_ref[...], mask8) > 0.5
        for g in range(kvh):
            s = _nt(q_ref[g], kk[:, dh * g:dh * (g + 1)].astype(BF16))
            if bias is not None:
                s = s + bias[g]
            s = jnp.where(keep, s, NEG)
            m_old = m_sc[g]
            m_new = jnp.maximum(m_old, jnp.max(s, axis=-1, keepdims=True))
            alpha = jnp.exp(m_old - m_new)
            p = jnp.exp(s - m_new)
            l_sc[g] = alpha * l_sc[g] + jnp.sum(p, axis=-1, keepdims=True)
            acc_sc[g] = alpha * acc_sc[g] + _nn(p.astype(BF16), vv[:, dh * g:dh * (g + 1)].astype(BF16))
            m_sc[g] = m_new

    kk = jnp.concatenate([p_[...] for p_ in k_pages], axis=0)
    vv = jnp.concatenate([p_[...] for p_ in v_pages], axis=0)

    @pl.when(c < nch - 1)
    def _():
        attend(kk, vv, mask_ref[...], None)

    @pl.when(c == nch - 1)
    def _():
        zeros = jnp.zeros((nrow, (npg - 1) * page), F32)
        attend(kk, vv, mask_ref[...], [jnp.concatenate([zeros, bl_ref[g]], axis=1) for g in range(kvh)])
        attend(kn_ref[...], vn_ref[...], maskn_ref[:, :page], [bn_ref[g] for g in range(kvh)])
        for g in range(kvh):
            o_ref[g] = (acc_sc[g] / l_sc[g]).astype(o_ref.dtype)


def _dsa_sample_attn(cache_k, cache_v, layer, page_table, q_s, mask8, bias_last, bias_new, expand, k_new, v_new,
                     *, kvh, dh):
    nseq, npages = page_table.shape
    page = cache_k.shape[2]
    npg = PAGES_PER_STEP
    nch = npages // npg
    chunk = npg * page
    nrow = q_s.shape[2]
    kspec = [pl.BlockSpec((None, None, page, kvh * dh), lambda b, c, pt, i=i: (layer, pt[b, c * npg + i], 0, 0))
             for i in range(npg)]
    const3 = lambda shp: pl.BlockSpec(shp, lambda b, c, pt: (0, 0, 0))
    grid_spec = pltpu.PrefetchScalarGridSpec(
        num_scalar_prefetch=1, grid=(nseq, nch),
        in_specs=kspec + kspec + [
            pl.BlockSpec((None, kvh, nrow, dh), lambda b, c, pt: (b, 0, 0, 0)),
            pl.BlockSpec((None, 8, chunk), lambda b, c, pt: (b, 0, c)),
            pl.BlockSpec((None, 8, chunk), lambda b, c, pt: (b, 0, nch)),
            const3(bias_last.shape), const3(bias_new.shape),
            pl.BlockSpec(expand.shape, lambda b, c, pt: (0, 0)),
            pl.BlockSpec((None, page, kvh * dh), lambda b, c, pt: (b, 0, 0)),
            pl.BlockSpec((None, page, kvh * dh), lambda b, c, pt: (b, 0, 0))],
        out_specs=pl.BlockSpec((None, kvh, nrow, dh), lambda b, c, pt: (b, 0, 0, 0)),
        scratch_shapes=[pltpu.VMEM((kvh, nrow, 1), F32), pltpu.VMEM((kvh, nrow, 1), F32),
                        pltpu.VMEM((kvh, nrow, dh), F32)])
    return pl.pallas_call(
        functools.partial(_dsa_sample_body, npg=npg, page=page, nch=nch, kvh=kvh, dh=dh, nrow=nrow),
        grid_spec=grid_spec,
        out_shape=jax.ShapeDtypeStruct((nseq, kvh, nrow, dh), BF16),
        compiler_params=_params("parallel", "arbitrary"), name="dsa_sample_attn",
    )(page_table, *([cache_k] * npg), *([cache_v] * npg), q_s, mask8, mask8, bias_last, bias_new, expand,
      k_new, v_new)


def _merge_body(oa_ref, ob_ref, wa_ref, wb_ref, ga_ref, gb_ref, y_ref):
    ya = _nn(oa_ref[...], wa_ref[...])
    yb = _nn(ob_ref[...], wb_ref[...])
    y_ref[...] = (jax.nn.sigmoid(ga_ref[...]) * ya + jax.nn.sigmoid(gb_ref[...]) * yb).astype(y_ref.dtype)


def _merge(oa, ob, wa, wb, z, zoff, *, tm, tn):
    m, d = oa.shape[0], wa.shape[1]
    return pl.pallas_call(
        _merge_body, grid=(m // tm, d // tn),
        in_specs=[pl.BlockSpec((tm, oa.shape[1]), lambda i, j: (i, 0)),
                  pl.BlockSpec((tm, ob.shape[1]), lambda i, j: (i, 0)),
                  pl.BlockSpec((wa.shape[0], tn), lambda i, j: (0, j)),
                  pl.BlockSpec((wb.shape[0], tn), lambda i, j: (0, j)),
                  pl.BlockSpec((tm, tn), lambda i, j: (i, zoff["ga"] // tn + j)),
                  pl.BlockSpec((tm, tn), lambda i, j: (i, zoff["gb"] // tn + j))],
        out_specs=pl.BlockSpec((tm, tn), lambda i, j: (i, j)),
        out_shape=jax.ShapeDtypeStruct((m, d), BF16),
        compiler_params=_params("parallel", "arbitrary"), name="merge",
    )(oa, ob, wa, wb, z, z)


def _peer_select_body(q_ref, k1_ref, k2_ref, s1_ref, s2_ref, thr_ref, c_ref, v1_sc, v2_sc, cand_sc,
                      *, ph, nk, half):
    topk = PEER_TOPK

    def top_rows(x, dst):
        for r in range(topk):
            mx = jnp.max(x, axis=0, keepdims=True)
            dst[r:r + 1, :] = mx
            x = jnp.where(x == mx, -jnp.inf, x)

    for h in range(ph):
        s1 = _nt(k1_ref[h], q_ref[:, 2 * half * h:2 * half * h + half])
        s2 = _nt(k2_ref[h], q_ref[:, 2 * half * h + half:2 * half * (h + 1)])
        s1_ref[h] = s1
        s2_ref[h] = s2
        top_rows(s1, v1_sc)
        top_rows(s2, v2_sc)
        for r in range(topk):
            cand_sc[topk * r:topk * (r + 1), :] = v1_sc[r:r + 1, :] + v2_sc[...]
        cand = cand_sc[...]
        x = cand
        for r in range(topk):
            mx = jnp.max(x, axis=0, keepdims=True)
            if r == 0:
                m0 = mx
            x = jnp.where(x == mx, -jnp.inf, x)
        thr = mx
        zsum = jnp.sum(jnp.where(cand >= thr, jnp.exp(cand - m0), 0.0), axis=0, keepdims=True)
        thr_ref[h] = thr
        c_ref[h] = m0 + jnp.log(zsum)


def _peer_select(q, k1, k2, *, tm):
    m = q.shape[0]
    ph, nk, half = k1.shape
    out3 = lambda r: pl.BlockSpec((ph, r, tm), lambda i: (0, 0, i))
    return pl.pallas_call(
        functools.partial(_peer_select_body, ph=ph, nk=nk, half=half),
        grid=(m // tm,),
        in_specs=[pl.BlockSpec((tm, q.shape[1]), lambda i: (i, 0)),
                  pl.BlockSpec(k1.shape, lambda i: (0, 0, 0)), pl.BlockSpec(k2.shape, lambda i: (0, 0, 0))],
        out_specs=[out3(nk), out3(nk), out3(1), out3(1)],
        out_shape=[jax.ShapeDtypeStruct((ph, nk, m), F32), jax.ShapeDtypeStruct((ph, nk, m), F32),
                   jax.ShapeDtypeStruct((ph, 1, m), F32), jax.ShapeDtypeStruct((ph, 1, m), F32)],
        scratch_shapes=[pltpu.VMEM((PEER_TOPK, tm), F32), pltpu.VMEM((PEER_TOPK, tm), F32),
                        pltpu.VMEM((PEER_TOPK * PEER_TOPK, tm), F32)],
        compiler_params=_params("parallel"), name="peer_select",
    )(q, k1, k2)


def _peer_dense_body(h_ref, u_ref, vt_ref, s1_ref, s2_ref, thr_ref, c_ref, o_ref, w_sc, *, ph, nk, eb):
    e = pl.program_id(1)

    @pl.when(e == 0)
    def _():
        o_ref[...] = jnp.zeros(o_ref.shape, F32)

    at = _nt(u_ref[...], h_ref[...])
    gel = 0.5 * at * (1.0 + lax.erf(at * (2.0 ** -0.5)))
    for a in range(eb // nk):
        w = jnp.zeros((nk, at.shape[1]), F32)
        for h in range(ph):
            cand = s1_ref[h, pl.ds(e * (eb // nk) + a, 1), :] + s2_ref[h]
            w = w + jnp.where(cand >= thr_ref[h], jnp.exp(cand - c_ref[h]), 0.0)
        w_sc[nk * a:nk * (a + 1), :] = (w * gel[nk * a:nk * (a + 1)]).astype(BF16)
    o_ref[...] += _nn(vt_ref[...], w_sc[...])


def _peer_dense(h2, u_b, vt_b, s1t, s2t, thr, cc, *, tm, eb):
    m, d = h2.shape
    ne = u_b.shape[0]
    ph, nk = s1t.shape[0], s1t.shape[1]
    tok3 = lambda r: pl.BlockSpec((ph, r, tm), lambda i, e: (0, 0, i))
    return pl.pallas_call(
        functools.partial(_peer_dense_body, ph=ph, nk=nk, eb=eb),
        grid=(m // tm, ne // eb),
        in_specs=[pl.BlockSpec((tm, d), lambda i, e: (i, 0)),
                  pl.BlockSpec((eb, d), lambda i, e: (e, 0)),
                  pl.BlockSpec((d, eb), lambda i, e: (0, e)),
                  tok3(nk), tok3(nk), tok3(1), tok3(1)],
        out_specs=pl.BlockSpec((d, tm), lambda i, e: (0, i)),
        out_shape=jax.ShapeDtypeStruct((d, m), F32),
        scratch_shapes=[pltpu.VMEM((eb, tm), BF16)],
        compiler_params=_params("parallel", "arbitrary"), name="peer_dense",
    )(h2, u_b, vt_b, s1t, s2t, thr, cc)


def _z_layout(sizes, tn):
    padded = {k: -(-v // LANE) * LANE for k, v in sizes.items()}
    order = sorted(sizes, key=lambda k: -padded[k])
    off, o = {}, 0
    for k in order:
        assert o % padded[k] == 0, (k, o, padded[k])
        off[k] = o
        o += padded[k]
    total = -(-o // tn) * tn
    return order, padded, off, total


def _t5_bucket(n, n_buckets):
    exact = n_buckets // 2
    nf = jnp.maximum(n, exact).astype(F32)
    large = exact + (jnp.log(nf / exact) / math.log(MAX_DISTANCE / exact) * (n_buckets - exact)).astype(jnp.int32)
    return jnp.where(n < exact, n, jnp.minimum(large, n_buckets - 1))


def _bias_by_distance(table):
    d = jnp.arange(MAX_DISTANCE + 1, dtype=jnp.int32)
    f = jnp.take(table, _t5_bucket(d, table.shape[0]), axis=0).astype(F32)
    return f - f[MAX_DISTANCE][None, :]


def _layer(l, xp, xs, c_all, cache_ckv, cache_krope, cache_k, cache_v, cache_idx_k, page_table, w_in, g_cq, w_uq,
           mla_gq_nope, mla_gq_pair, g_ckv, w_ukv, mla_gk_nope, mla_gk_pair, dsa_gq, dsa_gk, t5_table, w_proj_a,
           w_proj_b, w_out, g_attn, g_ffn, ada_w, ada_b, peer_wq, peer_k1, peer_k2, peer_u, peer_v):
    batch, seq, d = xp.shape
    nseq, tok, _ = xs.shape
    npages = page_table.shape[1]
    page = cache_ckv.shape[2]
    past = npages * page
    rq, rk = g_cq.shape[-1], g_ckv.shape[-1]
    rope_dim = cache_krope.shape[-1]
    half = rope_dim // 2
    heads = w_uq.shape[-1] // (MLA_NOPE + rope_dim)
    kvh, dh = cache_k.shape[-2], cache_k.shape[-1]
    dsa_heads = t5_table.shape[1]
    grp = dsa_heads // kvh
    di = cache_idx_k.shape[-1]
    n_in = w_in.shape[-1]
    idx_heads = (n_in - (rq + rk + rope_dim + dsa_heads * dh + 2 * kvh * dh + di + 2 * d)) // (di + 1)
    ph, nk, phalf = peer_k1.shape[1:]
    mp, ms = batch * seq, nseq * tok
    assert MLA_NOPE == LANE and MLA_VDIM == LANE and dh == LANE and di == LANE and page == LANE
    assert 2 * half <= LANE and idx_heads <= LANE and npages % PAGES_PER_STEP == 0

    nc = c_all.shape[0]
    mods = _matmul(c_all, ada_w[l], tm=nc, tn=_tile(6 * d, 512), out_dtype=F32, bias=ada_b[l].reshape(1, 6 * d),
                   pre_silu=True, name="adaln")
    mod_p = [mods[:batch, d * k:d * (k + 1)].reshape(batch, 1, d) for k in range(6)]
    mod_s = [jnp.repeat(mods[batch:batch + nseq, d * k:d * (k + 1)], tok, axis=0) for k in range(6)]

    sizes = dict(cq=rq, ckv=rk, kr=rope_dim, qd=dsa_heads * dh, kd=kvh * dh, vd=kvh * dh,
                 qi=idx_heads * di, ki=di, wi=idx_heads, ga=d, gb=d)
    split_order = ["cq", "ckv", "kr", "qd", "kd", "vd", "qi", "ki", "wi", "ga", "gb"]
    tn_z = 512
    order, padded, zoff, nz = _z_layout(sizes, tn_z)
    cuts, o = {}, 0
    for k in split_order:
        cuts[k] = (o, o + sizes[k])
        o += sizes[k]
    w_l = w_in[l]
    cols = [jnp.pad(w_l[:, cuts[k][0]:cuts[k][1]], ((0, 0), (0, padded[k] - sizes[k]))) for k in order]
    used = sum(padded.values())
    if nz > used:
        cols.append(jnp.zeros((d, nz - used), F32))
    w_z = jnp.concatenate(cols, axis=1).astype(BF16)

    wuq_p = jnp.pad(w_uq[l].reshape(rq, heads, MLA_NOPE + rope_dim),
                    ((0, 0), (0, 0), (0, 2 * LANE - MLA_NOPE - rope_dim))).reshape(rq, heads * 2 * LANE).astype(BF16)
    wukv = w_ukv[l].reshape(rk, heads, MLA_NOPE + MLA_VDIM)
    wuk = wukv[:, :, :MLA_NOPE].reshape(rk, heads * MLA_NOPE).astype(BF16)
    wuv = wukv[:, :, MLA_NOPE:].reshape(rk, heads * MLA_VDIM).astype(BF16)
    scale = (MLA_NOPE + rope_dim) ** -0.5
    pad_rot = lambda g: jnp.pad(jnp.concatenate([g, g]), (0, LANE - rope_dim))
    gn = (mla_gq_nope[l] * mla_gk_nope[l] * scale).reshape(1, LANE)
    gr = (pad_rot(mla_gq_pair[l]) * pad_rot(mla_gk_pair[l]) * scale).reshape(1, LANE)
    wa, wb, wo = w_proj_a[l].astype(BF16), w_proj_b[l].astype(BF16), w_out[l].astype(BF16)
    wq_peer = peer_wq[l].astype(BF16)
    k1_b, k2_b = peer_k1[l].astype(BF16), peer_k2[l].astype(BF16)
    u_b = peer_u[l].astype(BF16)
    vt_b = peer_v[l].astype(BF16).T

    fdist = _bias_by_distance(t5_table)

    def front(x2d, mod, rows_per_seq, pos, want_kv):
        h = _norm_mod(x2d, g_attn[l], mod[1], mod[0], rows_per_seq=rows_per_seq)
        m = x2d.shape[0]
        z = _matmul(h, w_z, tm=_tile(m, 512), tn=tn_z, out_dtype=F32, name="w_in")
        tabs = _rope_tables(pos, half)
        qf = _mla_q(z, zoff, g_cq[l], wuq_p, tabs, gn, gr, heads=heads, half=half, rows_per_seq=rows_per_seq)
        post = _post_proj(z, zoff, g_ckv[l], tabs, dsa_gq[l], dsa_gk[l], wuk, wuv, heads=heads, half=half, kvh=kvh,
                          dh=dh, dsa_heads=dsa_heads, rows_per_seq=rows_per_seq, want_kv=want_kv)
        return z, qf, post

    def back(x2d, oa, ob, z, mod, rows_per_seq):
        m = x2d.shape[0]
        tm = _tile(m, 512)
        y = _merge(oa, ob, wa, wb, z, zoff, tm=tm, tn=_tile(d, 512))
        attn = _matmul(y, wo, tm=tm, tn=_tile(d, 512), out_dtype=F32, name="w_out")
        x1, h2 = _norm_mod(x2d, g_ffn[l], mod[4], mod[3], rows_per_seq=rows_per_seq, y=attn, gate=mod[2])
        q = _matmul(h2, wq_peer, tm=tm, tn=_tile(wq_peer.shape[1], 512), out_dtype=BF16, name="peer_q")
        s1t, s2t, thr, cc = _peer_select(q, k1_b, k2_b, tm=_tile(m, 256))
        po = _peer_dense(h2, u_b, vt_b, s1t, s2t, thr, cc, tm=tm, eb=2 * nk)
        return _residual_t(x1, po, mod[5], rows_per_seq=rows_per_seq)

    xp2 = xp.reshape(mp, d)
    zp, qf_p, (ckv_p, krope_p, kdn_p, qdn_p, kf_p, vm_p) = front(xp2, mod_p, seq, jnp.arange(seq, dtype=jnp.int32), True)
    oa_p = _mla_prompt_attn(qf_p, kf_p, vm_p, batch=batch, seq=seq, heads=heads)
    sc_p = _index_prompt(zp, zoff, batch=batch, seq=seq, idx_heads=idx_heads, di=di)
    tq = LANE
    mask3 = _topk_mask(sc_p, min(TOPK_MAX, seq // 4), tr=tq, tiled=tq)
    ii = jnp.arange(tq, dtype=jnp.int32)
    dist = ii[:, None] - ii[None, :]
    tiles = jnp.stack([jnp.take(fdist, jnp.clip(dist + off, 0, MAX_DISTANCE), axis=0) for off in (0, tq)], axis=0)
    bias_tiles = jnp.transpose(tiles, (3, 0, 1, 2))
    ob_p = _dsa_prompt_attn(qdn_p, kdn_p, zp, zoff, mask3, bias_tiles, batch=batch, seq=seq, kvh=kvh, grp=grp, dh=dh,
                            tq=tq)
    yp = back(xp2, oa_p, ob_p, zp, mod_p, seq)

    xs2 = xs.reshape(ms, d)
    pos_s = past + jnp.arange(ms, dtype=jnp.int32) % tok
    zs, qf_s, (ckv_s, krope_s, kdn_s, qdn_s) = front(xs2, mod_s, ms, pos_s, False)
    qf4 = qf_s.reshape(nseq, tok, heads, 2 * LANE)
    eye = jnp.eye(heads, dtype=BF16)
    qexp = jnp.einsum("bthd,hg->bthgd", qf4[..., :MLA_NOPE], eye).reshape(nseq, tok * heads, heads * MLA_NOPE)
    qrot = qf4[..., MLA_NOPE:MLA_NOPE + rope_dim].reshape(nseq, tok * heads, rope_dim)
    pad_new = lambda a: jnp.pad(a.reshape(nseq, tok, a.shape[-1]), ((0, 0), (0, page - tok), (0, 0)))
    oa_s = _mla_sample_attn(cache_ckv, cache_krope, l, page_table, qexp, qrot, wuk.T, wuv, pad_new(ckv_s),
                            pad_new(krope_s), heads=heads, tok=tok, half=half)
    ki_s = zs[:, zoff["ki"]:zoff["ki"] + di]
    vd_s = zs[:, zoff["vd"]:zoff["vd"] + kvh * dh]
    qi_s = zs[:, zoff["qi"]:zoff["qi"] + idx_heads * di].astype(BF16).reshape(nseq, tok * idx_heads, di)
    cscale = (di ** -0.5) * (idx_heads ** -0.5)
    wi_s = (zs[:, zoff["wi"]:zoff["wi"] + idx_heads] * cscale).reshape(nseq, tok * idx_heads, 1)
    sc_s = _index_sample(cache_idx_k, l, page_table, qi_s, wi_s, pad_new(ki_s), tok=tok, idx_heads=idx_heads)
    lpad = sc_s.shape[-1]
    mask_s = _topk_mask(sc_s.reshape(ms, lpad), min(TOPK_MAX, (past + tok) // 4), tr=_tile(ms, 64), tiled=None)
    mask8 = jnp.pad(mask_s.reshape(nseq, tok, lpad), ((0, 0), (0, 8 - tok), (0, 0)))
    q_s = jnp.transpose(qdn_s.reshape(nseq, tok, kvh, grp, dh), (0, 2, 1, 3, 4)).reshape(nseq, kvh, tok * grp, dh)
    tt = jnp.arange(tok, dtype=jnp.int32)[:, None]
    mm = jnp.arange(page, dtype=jnp.int32)[None, :]
    per_head = lambda dd: jnp.transpose(jnp.take(fdist, jnp.clip(dd, 0, MAX_DISTANCE), axis=0).reshape(
        tok, page, kvh, grp), (2, 0, 3, 1)).reshape(kvh, tok * grp, page)
    bias_last = per_head(page + tt - mm)
    bias_new = per_head(tt - mm)
    expand = (jnp.arange(tok * grp, dtype=jnp.int32)[:, None] // grp == jnp.arange(8, dtype=jnp.int32)[None, :]
              ).astype(BF16)
    ck = cache_k.reshape(cache_k.shape[:3] + (kvh * dh,))
    cv = cache_v.reshape(cache_v.shape[:3] + (kvh * dh,))
    ob4 = _dsa_sample_attn(ck, cv, l, page_table, q_s, mask8, bias_last, bias_new, expand, pad_new(kdn_s),
                           pad_new(vd_s), kvh=kvh, dh=dh)
    ob_s = jnp.transpose(ob4.reshape(nseq, kvh, tok, grp, dh), (0, 2, 1, 3, 4)).reshape(ms, dsa_heads * dh)
    ys = back(xs2, oa_s.reshape(ms, heads * MLA_VDIM), ob_s, zs, mod_s, ms)

    st_p = (ckv_p.reshape(batch, seq, rk), krope_p.reshape(batch, seq, rope_dim),
            kdn_p.reshape(batch, seq, kvh, dh),
            zp[:, zoff["vd"]:zoff["vd"] + kvh * dh].reshape(batch, seq, kvh, dh),
            zp[:, zoff["ki"]:zoff["ki"] + di].reshape(batch, seq, di))
    st_s = (ckv_s.reshape(nseq, tok, rk), krope_s.reshape(nseq, tok, rope_dim), kdn_s.reshape(nseq, tok, kvh, dh),
            vd_s.reshape(nseq, tok, kvh, dh), ki_s.reshape(nseq, tok, di))
    return yp.reshape(batch, seq, d), ys.reshape(nseq, tok, d), st_p, st_s


def kernel(x_prompt, x_sample, c_prompt, c_sample, cache_ckv, cache_krope, cache_k, cache_v, cache_idx_k, page_table, w_in, g_cq, w_uq, mla_gq_nope, mla_gq_pair, g_ckv, w_ukv, mla_gk_nope, mla_gk_pair, dsa_gq, dsa_gk, t5_table, w_proj_a, w_proj_b, w_out, g_attn, g_ffn, ada_w, ada_b, peer_wq, peer_k1, peer_k2, peer_u, peer_v):
    depth = w_in.shape[0]
    nc = c_prompt.shape[0] + c_sample.shape[0]
    c_all = jnp.pad(jnp.concatenate([c_prompt, c_sample], axis=0), ((0, -nc % 8), (0, 0)))
    xp, xs = x_prompt, x_sample
    st_p, st_s = [], []
    for l in range(depth):
        xp, xs, sp, ss = _layer(l, xp, xs, c_all, cache_ckv, cache_krope, cache_k, cache_v, cache_idx_k, page_table,
                                w_in, g_cq, w_uq, mla_gq_nope, mla_gq_pair, g_ckv, w_ukv, mla_gk_nope, mla_gk_pair,
                                dsa_gq, dsa_gk, t5_table, w_proj_a, w_proj_b, w_out, g_attn, g_ffn, ada_w, ada_b,
                                peer_wq, peer_k1, peer_k2, peer_u, peer_v)
        st_p.append(sp)
        st_s.append(ss)
    stack = lambda states: tuple(jnp.stack([s[i] for s in states]) for i in range(len(states[0])))
    return (xp, xs) + stack(st_p) + stack(st_s)
```
